```python
import jax, jax.numpy as jnp
from jax import lax
import numpy as np

D_MODEL = 1024
BATCH = 8
SEQ = 2048
DEPTH = 1

HEAD_DIM = 64
N_ATTN_HEADS = 8
ATTN_WIDTH = N_ATTN_HEADS * HEAD_DIM
Q_BLOCK = 128
N_SGU_GROUPS = 8
SGU_GROUP_DIM = 64
SGU_WIDTH = N_SGU_GROUPS * SGU_GROUP_DIM
CHUNK = 128
N_BRANCHES = 2
OFF_Q = 0
OFF_K = OFF_Q + ATTN_WIDTH
OFF_V = OFF_K + ATTN_WIDTH
OFF_F = OFF_V + ATTN_WIDTH
OFF_UG = OFF_F + N_ATTN_HEADS
OFF_VG = OFF_UG + SGU_WIDTH
OFF_GATE = OFF_VG + SGU_WIDTH
IN_WIDTH = OFF_GATE + N_BRANCHES * D_MODEL
PEER_HEADS = 8
PEER_N_KEYS = 128
PEER_N_EXPERTS = PEER_N_KEYS * PEER_N_KEYS
PEER_KEY_DIM = 256
PEER_HALF = PEER_KEY_DIM // 2
PEER_TOPK = 16
PEER_TOKEN_BLOCK = 128
EPS = 1e-6

kernel_name = "hybrid_fox_sgu_peer_block"


def rmsnorm(x, g):
    x32 = x.astype(jnp.float32)
    r = x32 * lax.rsqrt(jnp.mean(x32 * x32, axis=-1, keepdims=True) + EPS)
    return (r * g.astype(jnp.float32)).astype(x.dtype)


def forgetting_attention(q, k, v, log_f):
    B, H, S, Dh = q.shape
    c = jnp.cumsum(log_f, axis=-1)
    scale = Dh ** -0.5
    neg = jnp.finfo(jnp.float32).min
    outs = []
    for i in range(S // Q_BLOCK):
        q0 = i * Q_BLOCK
        end = q0 + Q_BLOCK
        qb = q[:, :, q0:end]
        kb = k[:, :, :end]
        vb = v[:, :, :end]
        logits = jnp.einsum('bhqd,bhkd->bhqk', qb, kb, preferred_element_type=jnp.float32) * scale
        logits = logits + c[:, :, q0:end, None] - c[:, :, None, :end]
        causal = (q0 + jnp.arange(Q_BLOCK))[:, None] >= jnp.arange(end)[None, :]
        logits = jnp.where(causal[None, None], logits, neg)
        p = jax.nn.softmax(logits, axis=-1)
        outs.append(jnp.einsum('bhqk,bhkd->bhqd', p.astype(vb.dtype), vb))
    return jnp.concatenate(outs, axis=2)


def chunked_spatial_gating(u, v, ln_g, ln_b, w_s, b_s):
    B, S, W = v.shape
    v32 = v.astype(jnp.float32)
    mu = jnp.mean(v32, axis=-1, keepdims=True)
    var = jnp.mean(jnp.square(v32 - mu), axis=-1, keepdims=True)
    vn = ((v32 - mu) * lax.rsqrt(var + EPS) * ln_g.astype(jnp.float32) + ln_b.astype(jnp.float32)).astype(v.dtype)
    vn = vn.reshape(B, S // CHUNK, CHUNK, N_SGU_GROUPS, SGU_GROUP_DIM)
    mask = jnp.tril(jnp.ones((CHUNK, CHUNK), dtype=w_s.dtype))
    w = w_s * mask[None]
    z = jnp.einsum('gts,bnsgc->bntgc', w, vn) + b_s.T[None, None, :, :, None]
    return u * z.reshape(B, S, W).astype(u.dtype)


def peer_ffn(h, w_query, sub_keys, expert_u, expert_v):
    B, S, D = h.shape
    T = B * S
    xt = h.reshape(T, D)
    q = (xt @ w_query).reshape(T, PEER_HEADS, 2, PEER_HALF)
    scores = jnp.einsum('thpk,hpnk->thpn', q, sub_keys, preferred_element_type=jnp.float32)
    s_top, i_top = lax.top_k(scores, PEER_TOPK)
    cand_s = s_top[:, :, 0, :, None] + s_top[:, :, 1, None, :]
    cand_i = i_top[:, :, 0, :, None] * PEER_N_KEYS + i_top[:, :, 1, None, :]
    cand_s = cand_s.reshape(T, PEER_HEADS, PEER_TOPK * PEER_TOPK)
    cand_i = cand_i.reshape(T, PEER_HEADS, PEER_TOPK * PEER_TOPK)
    best_s, best_pos = lax.top_k(cand_s, PEER_TOPK)
    expert_idx = jnp.take_along_axis(cand_i, best_pos, axis=-1)
    gate = jax.nn.softmax(best_s, axis=-1)
    HK = PEER_HEADS * PEER_TOPK
    n_blk = T // PEER_TOKEN_BLOCK

    def block_fn(args):
        xb, idx, g = args
        u = expert_u[idx]
        a = jax.nn.gelu(jnp.einsum('td,tkd->tk', xb, u, preferred_element_type=jnp.float32), approximate=False)
        vv = expert_v[idx]
        return jnp.einsum('tk,tkd->td', (g * a).astype(vv.dtype), vv)

    out = lax.map(block_fn, (xt.reshape(n_blk, PEER_TOKEN_BLOCK, D),
                             expert_idx.reshape(n_blk, PEER_TOKEN_BLOCK, HK),
                             gate.reshape(n_blk, PEER_TOKEN_BLOCK, HK)))
    return out.reshape(B, S, D)


def setup_inputs(seed: int = 0) -> dict:
    key = jax.random.key(seed)
    ks = jax.random.split(key, 20)
    L = DEPTH
    f32 = jnp.float32

    def nrm(k, shape, scale):
        return jax.random.normal(k, shape, f32) * scale

    x = jax.random.normal(ks[0], (BATCH, SEQ, D_MODEL), f32)
    norm_mix_g = 1.0 + nrm(ks[1], (L, D_MODEL), 0.01)
    w_in = nrm(ks[2], (L, D_MODEL, IN_WIDTH), D_MODEL ** -0.5)
    b_f = jax.random.uniform(ks[3], (L, N_ATTN_HEADS), f32, minval=1.0, maxval=4.0)
    sgu_ln_g = 1.0 + nrm(ks[4], (L, SGU_WIDTH), 0.01)
    sgu_ln_b = nrm(ks[5], (L, SGU_WIDTH), 0.01)
    w_s = nrm(ks[6], (L, N_SGU_GROUPS, CHUNK, CHUNK), CHUNK ** -0.5)
    b_s = 1.0 + nrm(ks[7], (L, N_SGU_GROUPS, CHUNK), 0.01)
    w_branch_a = nrm(ks[8], (L, ATTN_WIDTH, D_MODEL), ATTN_WIDTH ** -0.5)
    w_branch_b = nrm(ks[9], (L, SGU_WIDTH, D_MODEL), SGU_WIDTH ** -0.5)
    w_out = nrm(ks[10], (L, D_MODEL, D_MODEL), D_MODEL ** -0.5)
    norm_ffn_g = 1.0 + nrm(ks[11], (L, D_MODEL), 0.01)
    w_query = nrm(ks[12], (L, D_MODEL, PEER_HEADS * PEER_KEY_DIM), D_MODEL ** -0.5)
    sub_keys = nrm(ks[13], (L, PEER_HEADS, 2, PEER_N_KEYS, PEER_HALF), PEER_HALF ** -0.5)
    expert_u = nrm(ks[14], (L, PEER_N_EXPERTS, D_MODEL), D_MODEL ** -0.5)
    expert_v = nrm(ks[15], (L, PEER_N_EXPERTS, D_MODEL), 0.5)
    norm_final_g = 1.0 + nrm(ks[16], (D_MODEL,), 0.01)
    return {"x": x, "norm_mix_g": norm_mix_g, "w_in": w_in, "b_f": b_f,
            "sgu_ln_g": sgu_ln_g, "sgu_ln_b": sgu_ln_b, "w_s": w_s, "b_s": b_s,
            "w_branch_a": w_branch_a, "w_branch_b": w_branch_b, "w_out": w_out,
            "norm_ffn_g": norm_ffn_g, "w_query": w_query, "sub_keys": sub_keys,
            "expert_u": expert_u, "expert_v": expert_v, "norm_final_g": norm_final_g}


def reference(x, norm_mix_g, w_in, b_f, sgu_ln_g, sgu_ln_b, w_s, b_s, w_branch_a, w_branch_b,
              w_out, norm_ffn_g, w_query, sub_keys, expert_u, expert_v, norm_final_g):
    B, S, D = x.shape
    for l in range(DEPTH):
        h = rmsnorm(x, norm_mix_g[l])
        proj = h @ w_in[l]
        q = proj[..., OFF_Q:OFF_K].reshape(B, S, N_ATTN_HEADS, HEAD_DIM).transpose(0, 2, 1, 3)
        k = proj[..., OFF_K:OFF_V].reshape(B, S, N_ATTN_HEADS, HEAD_DIM).transpose(0, 2, 1, 3)
        v = proj[..., OFF_V:OFF_F].reshape(B, S, N_ATTN_HEADS, HEAD_DIM).transpose(0, 2, 1, 3)
        log_f = jax.nn.log_sigmoid(proj[..., OFF_F:OFF_UG].astype(jnp.float32)
                                   + b_f[l].astype(jnp.float32)).transpose(0, 2, 1)
        y_a = forgetting_attention(q, k, v, log_f).transpose(0, 2, 1, 3).reshape(B, S, ATTN_WIDTH)
        u_g = jax.nn.gelu(proj[..., OFF_UG:OFF_VG], approximate=False)
        v_g = jax.nn.gelu(proj[..., OFF_VG:OFF_GATE], approximate=False)
        y_b = chunked_spatial_gating(u_g, v_g, sgu_ln_g[l], sgu_ln_b[l], w_s[l], b_s[l])
        gates = jax.nn.sigmoid(proj[..., OFF_GATE:].astype(jnp.float32)).reshape(B, S, N_BRANCHES, D)
        merged = (gates[:, :, 0] * (y_a @ w_branch_a[l]).astype(jnp.float32)
                  + gates[:, :, 1] * (y_b @ w_branch_b[l]).astype(jnp.float32)).astype(x.dtype)
        x = x + merged @ w_out[l]
        h2 = rmsnorm(x, norm_ffn_g[l])
        x = x + peer_ffn(h2, w_query[l], sub_keys[l], expert_u[l], expert_v[l])
    return rmsnorm(x, norm_final_g)
```

```python
import functools
import math

import jax
import jax.numpy as jnp
from jax import lax
from jax.experimental import pallas as pl
from jax.experimental.pallas import tpu as pltpu

_EPS = 1e-6
_TOPK = 16
_LANES = 128
_SUBLANES = 8
_INV_SQRT2 = 1.0 / math.sqrt(2.0)
_NEG_BIG = -1e30
_VMEM_LIMIT_BYTES = 56 * 1024 * 1024
_G_PITCH_PAD = 4

_BF16 = jnp.bfloat16
_F32 = jnp.float32
_NT = (((1,), (1,)), ((), ()))


def _gelu(x):
    return 0.5 * x * (1.0 + lax.erf(x * _INV_SQRT2))


def _rms(x, g):
    ms = jnp.mean(x * x, axis=-1, keepdims=True)
    return x * lax.rsqrt(ms + _EPS) * g


def _dot(a, b):
    return jnp.dot(a, b, preferred_element_type=_F32)


def _dot_nt(a, b):
    return lax.dot_general(a, b, _NT, preferred_element_type=_F32)


def _mix_in_kernel(x_ref, g_ref, wqkv_ref, wft_ref, bf_ref, wug_ref, wvg_ref, wgate_ref,
                   lng_ref, lnb_ref, ws_ref, bexp_ref, wbb_ref,
                   q_ref, k_ref, v_ref, lf_ref, ga_ref, mb_ref, *, attn_w, d_model, chunk, q_scale):
    tm = x_ref.shape[0]
    h = _rms(x_ref[...], g_ref[...]).astype(_BF16)

    qkv = _dot(h, wqkv_ref[...])
    q_ref[...] = (qkv[:, :attn_w] * q_scale).astype(_BF16)
    k_ref[...] = qkv[:, attn_w:2 * attn_w].astype(_BF16)
    v_ref[...] = qkv[:, 2 * attn_w:].astype(_BF16)

    f_t = _dot_nt(wft_ref[...], h) + bf_ref[...]
    lf_ref[...] = jnp.minimum(f_t, 0.0) - jnp.log1p(jnp.exp(-jnp.abs(f_t)))

    ug = _gelu(_dot(h, wug_ref[...]))
    vg = _gelu(_dot(h, wvg_ref[...]))
    mu = jnp.mean(vg, axis=-1, keepdims=True)
    var = jnp.mean(jnp.square(vg - mu), axis=-1, keepdims=True)
    vn = (vg - mu) * lax.rsqrt(var + _EPS) * lng_ref[...] + lnb_ref[...]

    n_groups = ws_ref.shape[0]
    row = lax.broadcasted_iota(jnp.int32, (chunk, chunk), 0)
    col = lax.broadcasted_iota(jnp.int32, (chunk, chunk), 1)
    tril = (row >= col).astype(_F32)
    wm = [(ws_ref[g] * tril).astype(_BF16) for g in range(n_groups)]
    lane = lax.broadcasted_iota(jnp.int32, (chunk, _LANES), 1)
    lo_half = lane < (_LANES // 2)
    zs = []
    for c in range(tm // chunk):
        slabs = []
        for p in range(n_groups // 2):
            vp = vn[c * chunk:(c + 1) * chunk, p * _LANES:(p + 1) * _LANES]
            lo = jnp.where(lo_half, vp, 0.0).astype(_BF16)
            hi = jnp.where(lo_half, 0.0, vp).astype(_BF16)
            slabs.append(_dot(wm[2 * p], lo) + _dot(wm[2 * p + 1], hi))
        zs.append(jnp.concatenate(slabs, axis=1) + bexp_ref[...])
    z = jnp.concatenate(zs, axis=0) if len(zs) > 1 else zs[0]
    yb = (ug * z).astype(_BF16)

    gate = _dot(h, wgate_ref[...])
    ga_ref[...] = jax.nn.sigmoid(gate[:, :d_model]).astype(_BF16)
    mb = jax.nn.sigmoid(gate[:, d_model:]) * _dot(yb, wbb_ref[...])
    mb_ref[...] = mb.astype(_BF16)


def _mix_in(x2, g_mix, wqkv, wft, bf, wug, wvg, wgate, lng, lnb, ws, bexp, wbb, *, tm, chunk, q_scale):
    t, d = x2.shape
    attn_w = wqkv.shape[1] // 3
    n_heads = wft.shape[0]
    full = lambda a: pl.BlockSpec(a.shape, lambda i: (0,) * a.ndim)
    kern = functools.partial(_mix_in_kernel, attn_w=attn_w, d_model=d, chunk=chunk, q_scale=q_scale)
    return pl.pallas_call(
        kern,
        grid=(t // tm,),
        in_specs=[pl.BlockSpec((tm, d), lambda i: (i, 0)), full(g_mix), full(wqkv), full(wft), full(bf),
                  full(wug), full(wvg), full(wgate), full(lng), full(lnb), full(ws), full(bexp), full(wbb)],
        out_specs=[pl.BlockSpec((tm, attn_w), lambda i: (i, 0)),
                   pl.BlockSpec((tm, attn_w), lambda i: (i, 0)),
                   pl.BlockSpec((tm, attn_w), lambda i: (i, 0)),
                   pl.BlockSpec((n_heads, tm), lambda i: (0, i)),
                   pl.BlockSpec((tm, d), lambda i: (i, 0)),
                   pl.BlockSpec((tm, d), lambda i: (i, 0))],
        out_shape=[jax.ShapeDtypeStruct((t, attn_w), _BF16),
                   jax.ShapeDtypeStruct((t, attn_w), _BF16),
                   jax.ShapeDtypeStruct((t, attn_w), _BF16),
                   jax.ShapeDtypeStruct((n_heads, t), _F32),
                   jax.ShapeDtypeStruct((t, d), _BF16),
                   jax.ShapeDtypeStruct((t, d), _BF16)],
        compiler_params=pltpu.CompilerParams(dimension_semantics=("arbitrary",),
                                             vmem_limit_bytes=_VMEM_LIMIT_BYTES),
        name="mix_in",
    )(x2, g_mix, wqkv, wft, bf, wug, wvg, wgate, lng, lnb, ws, bexp, wbb)


def _cumsum_kernel(lf_ref, c_ref):
    n_rows, s = lf_ref.shape
    r = lax.broadcasted_iota(jnp.int32, (_LANES, _LANES), 0)
    c = lax.broadcasted_iota(jnp.int32, (_LANES, _LANES), 1)
    upper = (r <= c).astype(_F32)
    carry = jnp.zeros((n_rows, 1), _F32)
    for blk in range(s // _LANES):
        xb = lf_ref[:, blk * _LANES:(blk + 1) * _LANES]
        cb = jnp.dot(xb, upper, preferred_element_type=_F32, precision=lax.Precision.HIGHEST) + carry
        c_ref[:, blk * _LANES:(blk + 1) * _LANES] = cb
        carry = cb[:, _LANES - 1:_LANES]


def _cumsum(lf, *, seq):
    n_rows, t = lf.shape
    return pl.pallas_call(
        _cumsum_kernel,
        grid=(t // seq,),
        in_specs=[pl.BlockSpec((n_rows, seq), lambda b: (0, b))],
        out_specs=pl.BlockSpec((n_rows, seq), lambda b: (0, b)),
        out_shape=jax.ShapeDtypeStruct((n_rows, t), _F32),
        compiler_params=pltpu.CompilerParams(dimension_semantics=("arbitrary",)),
        name="cumsum_logf",
    )(lf)


def _attn_kernel(q_ref, k_ref, v_ref, c_ref, o_ref, *, tq, tk, head_dim):
    p = pl.program_id(1)
    qi = pl.program_id(2)
    q = q_ref[0]
    lane = lax.broadcasted_iota(jnp.int32, (tq, _LANES), 1)
    lo_half = lane < head_dim
    zero = jnp.zeros_like(q)
    rows = qi * tq + lax.broadcasted_iota(jnp.int32, (tq, tk), 0)
    cols0 = lax.broadcasted_iota(jnp.int32, (tq, tk), 1)
    n_kv = (qi * tq + tq + tk - 1) // tk

    def one_head(qh, head_row):
        def body(j, carry):
            m, l, acc = carry
            start = pl.multiple_of(j * tk, tk)
            kb = k_ref[0, pl.ds(start, tk), :]
            vb = v_ref[0, pl.ds(start, tk), :]
            cs = c_ref[pl.ds(head_row, 1), pl.ds(start, tk)]
            s = _dot_nt(qh, kb) - cs
            s = jnp.where(rows >= cols0 + j * tk, s, _NEG_BIG)
            m_new = jnp.maximum(m, jnp.max(s, axis=-1, keepdims=True))
            pexp = jnp.exp(s - m_new)
            alpha = jnp.exp(m - m_new)
            l_new = alpha * l + jnp.sum(pexp, axis=-1, keepdims=True)
            acc_new = alpha * acc + _dot(pexp.astype(_BF16), vb)
            return m_new, l_new, acc_new

        init = (jnp.full((tq, 1), _NEG_BIG, _F32), jnp.zeros((tq, 1), _F32), jnp.zeros((tq, _LANES), _F32))
        _, l, acc = lax.fori_loop(0, n_kv, body, init)
        return acc / l

    o_lo = one_head(jnp.where(lo_half, q, zero), 2 * p)
    o_hi = one_head(jnp.where(lo_half, zero, q), 2 * p + 1)
    o_ref[0] = jnp.where(lo_half, o_lo, o_hi).astype(_BF16)


def _attention(q, k, v, c, *, batch, seq, head_dim, tq, tk):
    t, attn_w = q.shape
    n_pairs = attn_w // _LANES
    n_heads = c.shape[0]
    q3, k3, v3 = (a.reshape(batch, seq, attn_w) for a in (q, k, v))
    kern = functools.partial(_attn_kernel, tq=tq, tk=tk, head_dim=head_dim)
    out = pl.pallas_call(
        kern,
        grid=(batch, n_pairs, seq // tq),
        in_specs=[pl.BlockSpec((1, tq, _LANES), lambda b, p, i: (b, i, p)),
                  pl.BlockSpec((1, seq, _LANES), lambda b, p, i: (b, 0, p)),
                  pl.BlockSpec((1, seq, _LANES), lambda b, p, i: (b, 0, p)),
                  pl.BlockSpec((n_heads, seq), lambda b, p, i: (0, b))],
        out_specs=pl.BlockSpec((1, tq, _LANES), lambda b, p, i: (b, i, p)),
        out_shape=jax.ShapeDtypeStruct((batch, seq, attn_w), _BF16),
        compiler_params=pltpu.CompilerParams(dimension_semantics=("arbitrary", "arbitrary", "arbitrary"),
                                             vmem_limit_bytes=_VMEM_LIMIT_BYTES),
        name="forget_attention",
    )(q3, k3, v3, c)
    return out.reshape(t, attn_w)


def _extract_topk(vals, payload, n_out):
    n = vals.shape[0]
    pos = lax.broadcasted_iota(jnp.int32, vals.shape, 0)
    out_v, out_p = [], []
    for _ in range(n_out):
        m = jnp.max(vals, axis=0, keepdims=True)
        first = jnp.min(jnp.where(vals == m, pos, n), axis=0, keepdims=True)
        sel = pos == first
        out_v.append(m)
        out_p.append(jnp.max(jnp.where(sel, payload, -1), axis=0, keepdims=True))
        vals = jnp.where(sel, -jnp.inf, vals)
    return jnp.concatenate(out_v, axis=0), jnp.concatenate(out_p, axis=0)


def _pair_candidates(a0, a1, combine):
    half = _SUBLANES
    bc = lambda x, r: jnp.broadcast_to(x[r:r + 1], (half, x.shape[1]))
    parts = [combine(bc(a0, 0), a1[0:half]), combine(bc(a0, 0), a1[half:2 * half])]
    parts += [combine(bc(a0, a), a1[0:half]) for a in range(1, half)]
    parts.append(combine(a0[half:2 * half], bc(a1, 0)))
    return jnp.concatenate(parts, axis=0)


def _merge_route_kernel(ya_ref, ga_ref, mb_ref, x_ref, wba_ref, wout_ref, gffn_ref, wq_ref, sk_ref,
                        x1_ref, h2_ref, eidx_ref, gate_ref, gate_scr, eid_scr, *, n_keys):
    tm = x_ref.shape[0]
    n_heads = wq_ref.shape[0]
    ma = _dot(ya_ref[...], wba_ref[...])
    merged = (ga_ref[...].astype(_F32) * ma + mb_ref[...].astype(_F32)).astype(_BF16)
    x1 = x_ref[...] + _dot(merged, wout_ref[...])
    x1_ref[...] = x1
    h2_ref[...] = _rms(x1, gffn_ref[...]).astype(_BF16)

    def head_body(h, carry):
        qh = _dot(h2_ref[...], wq_ref[h])
        half = qh.shape[1] // 2
        s_t = [_dot_nt(sk_ref[h, pp], qh[:, pp * half:(pp + 1) * half].astype(_BF16)) for pp in range(2)]
        key_id = lax.broadcasted_iota(jnp.int32, (n_keys, _LANES), 0)
        for lg in range(tm // _LANES):
            ls = slice(lg * _LANES, (lg + 1) * _LANES)
            s0, i0 = _extract_topk(s_t[0][:, ls], key_id, _TOPK)
            s1, i1 = _extract_topk(s_t[1][:, ls], key_id, _TOPK)
            cand_s = _pair_candidates(s0, s1, lambda a, b: a + b)
            cand_e = _pair_candidates(i0, i1, lambda a, b: a * n_keys + b)
            best, eid = _extract_topk(cand_s, cand_e, _TOPK)
            ex = jnp.exp(best - best[0:1])
            gate = ex / jnp.sum(ex, axis=0, keepdims=True)
            r0 = pl.multiple_of(h * _TOPK, _TOPK)
            gate_scr[pl.ds(r0, _TOPK), ls] = gate
            eid_scr[pl.ds(r0, _TOPK), ls] = eid
        return carry

    lax.fori_loop(0, n_heads, head_body, 0)
    gate_ref[...] = gate_scr[...].T
    eidx_ref[...] = eid_scr[...].T


def _merge_route(ya, ga, mb, x2, wba, wout, gffn, wq, sk, *, tm):
    t, d = x2.shape
    attn_w = ya.shape[1]
    n_heads, _, n_keys, _ = sk.shape
    hk = n_heads * _TOPK
    full = lambda a: pl.BlockSpec(a.shape, lambda i: (0,) * a.ndim)
    kern = functools.partial(_merge_route_kernel, n_keys=n_keys)
    return pl.pallas_call(
        kern,
        grid=(t // tm,),
        in_specs=[pl.BlockSpec((tm, attn_w), lambda i: (i, 0)),
                  pl.BlockSpec((tm, d), lambda i: (i, 0)),
                  pl.BlockSpec((tm, d), lambda i: (i, 0)),
                  pl.BlockSpec((tm, d), lambda i: (i, 0)),
                  full(wba), full(wout), full(gffn), full(wq), full(sk)],
        out_specs=[pl.BlockSpec((tm, d), lambda i: (i, 0)),
                   pl.BlockSpec((tm, d), lambda i: (i, 0)),
                   pl.BlockSpec((tm, hk), lambda i: (i, 0)),
                   pl.BlockSpec((tm, hk), lambda i: (i, 0))],
        out_shape=[jax.ShapeDtypeStruct((t, d), _F32),
                   jax.ShapeDtypeStruct((t, d), _BF16),
                   jax.ShapeDtypeStruct((t, hk), jnp.int32),
                   jax.ShapeDtypeStruct((t, hk), _F32)],
        scratch_shapes=[pltpu.VMEM((hk, tm), _F32), pltpu.VMEM((hk, tm), jnp.int32)],
        compiler_params=pltpu.CompilerParams(dimension_semantics=("arbitrary",),
                                             vmem_limit_bytes=_VMEM_LIMIT_BYTES),
        name="merge_route",
    )(ya, ga, mb, x2, wba, wout, gffn, wq, sk)


def _peer_kernel(h2_ref, x1_ref, eidx_ref, gate_ref, u_ref, v_ref, gfin_ref, out_ref, gs_ref, acc_ref,
                 *, n_keys, pitch, final_norm):
    tb = h2_ref.shape[0]
    ec = u_ref.shape[0]
    e = pl.program_id(1)
    key_shift = n_keys.bit_length() - 1

    @pl.when(e == 0)
    def _build_gate_matrices():
        acc_ref[...] = jnp.zeros_like(acc_ref)
        sub = lax.broadcasted_iota(jnp.int32, (n_keys, eidx_ref.shape[1]), 0)

        def build(t, carry):
            er = eidx_ref[pl.ds(t, 1), :]
            gr = gate_ref[pl.ds(t, 1), :]
            p_t = jnp.where((er >> key_shift) == sub, gr, 0.0).astype(_BF16)
            q_t = jnp.where((er & (n_keys - 1)) == sub, 1.0, 0.0).astype(_BF16)
            gs_ref[pl.ds(t * pitch, n_keys), :] = _dot_nt(p_t, q_t)
            return carry

        lax.fori_loop(0, tb, build, 0)

    a = _dot_nt(h2_ref[...], u_ref[...])
    ws = []
    for ii in range(ec // n_keys):
        gi = gs_ref[pl.ds(e * (ec // n_keys) + ii, tb, stride=pitch), :]
        ws.append((gi * _gelu(a[:, ii * n_keys:(ii + 1) * n_keys])).astype(_BF16))
    w = jnp.concatenate(ws, axis=1) if len(ws) > 1 else ws[0]
    acc_ref[...] += _dot(w, v_ref[...])

    @pl.when(e == pl.num_programs(1) - 1)
    def _finish():
        y = x1_ref[...] + acc_ref[...]
        out_ref[...] = _rms(y, gfin_ref[...]) if final_norm else y


def _peer(h2, x1, eidx, gate, u_bf, v_bf, gfin, *, tb, ec, n_keys, final_norm):
    t, d = x1.shape
    n_exp = u_bf.shape[0]
    hk = eidx.shape[1]
    pitch = n_keys + _G_PITCH_PAD
    kern = functools.partial(_peer_kernel, n_keys=n_keys, pitch=pitch, final_norm=final_norm)
    return pl.pallas_call(
        kern,
        grid=(t // tb, n_exp // ec),
        in_specs=[pl.BlockSpec((tb, d), lambda i, e: (i, 0)),
                  pl.BlockSpec((tb, d), lambda i, e: (i, 0)),
                  pl.BlockSpec((tb, hk), lambda i, e: (i, 0)),
                  pl.BlockSpec((tb, hk), lambda i, e: (i, 0)),
                  pl.BlockSpec((ec, d), lambda i, e: (e, 0)),
                  pl.BlockSpec((ec, d), lambda i, e: (e, 0)),
                  pl.BlockSpec((1, d), lambda i, e: (0, 0))],
        out_specs=pl.BlockSpec((tb, d), lambda i, e: (i, 0)),
        out_shape=jax.ShapeDtypeStruct((t, d), _F32),
        scratch_shapes=[pltpu.VMEM((tb * pitch, n_keys), _F32), pltpu.VMEM((tb, d), _F32)],
        compiler_params=pltpu.CompilerParams(dimension_semantics=("arbitrary", "arbitrary"),
                                             vmem_limit_bytes=_VMEM_LIMIT_BYTES),
        name="peer_dense",
    )(h2, x1, eidx, gate, u_bf, v_bf, gfin)


def _tile(n, want):
    t = min(n, want)
    while n % t:
        t -= _LANES
    return t


def kernel(x, norm_mix_g, w_in, b_f, sgu_ln_g, sgu_ln_b, w_s, b_s, w_branch_a, w_branch_b, w_out, norm_ffn_g,
           w_query, sub_keys, expert_u, expert_v, norm_final_g):
    batch, seq, d = x.shape
    depth = w_in.shape[0]
    n_attn_heads = b_f.shape[1]
    attn_w = w_branch_a.shape[1]
    head_dim = attn_w // n_attn_heads
    sgu_w = w_branch_b.shape[1]
    n_groups, chunk = w_s.shape[1], w_s.shape[2]
    group_dim = sgu_w // n_groups
    peer_heads, _, n_keys, half = sub_keys.shape[1:]
    assert 2 * head_dim == _LANES and 2 * group_dim == _LANES and chunk == _LANES
    assert n_keys == _LANES and half == _LANES and peer_heads * _TOPK == _LANES
    assert seq % chunk == 0 and d % _LANES == 0
    off_f = 3 * attn_w
    off_ug = off_f + n_attn_heads
    off_vg = off_ug + sgu_w
    off_gate = off_vg + sgu_w
    assert w_in.shape[2] == off_gate + 2 * d

    t = batch * seq
    tm = _tile(seq, 256)
    x2 = x.reshape(t, d)
    for l in range(depth):
        wl = w_in[l]
        wqkv = wl[:, :off_f].astype(_BF16)
        wft = wl[:, off_f:off_ug].T.astype(_BF16)
        wug = wl[:, off_ug:off_vg].astype(_BF16)
        wvg = wl[:, off_vg:off_gate].astype(_BF16)
        wgate = wl[:, off_gate:].astype(_BF16)
        bexp = jnp.repeat(b_s[l].T, group_dim, axis=1)
        q, k, v, lf, ga, mb = _mix_in(
            x2, norm_mix_g[l][None], wqkv, wft, b_f[l][:, None], wug, wvg, wgate,
            sgu_ln_g[l][None], sgu_ln_b[l][None], w_s[l], bexp, w_branch_b[l].astype(_BF16),
            tm=tm, chunk=chunk, q_scale=head_dim ** -0.5)
        c = _cumsum(lf, seq=seq)
        ya = _attention(q, k, v, c, batch=batch, seq=seq, head_dim=head_dim, tq=tm, tk=tm)
        wq = w_query[l].reshape(d, peer_heads, 2 * half).transpose(1, 0, 2).astype(_BF16)
        x1, h2, eidx, gate = _merge_route(
            ya, ga, mb, x2, w_branch_a[l].astype(_BF16), w_out[l].astype(_BF16), norm_ffn_g[l][None],
            wq, sub_keys[l].astype(_BF16), tm=tm)
        x2 = _peer(h2, x1, eidx, gate, expert_u[l].astype(_BF16), expert_v[l].astype(_BF16),
                   norm_final_g[None], tb=tm, ec=_tile(n_keys * n_keys, 512), n_keys=n_keys,
                   final_norm=(l == depth - 1))
    return x2.reshape(batch, seq, d)
```

```python
import functools
import math

import jax
import jax.numpy as jnp
from jax import lax
from jax.experimental import pallas as pl
from jax.experimental.pallas import tpu as pltpu

_EPS = 1e-6
_TOPK = 16
_LANES = 128
_SUBLANES = 8
_INV_SQRT2 = 1.0 / math.sqrt(2.0)
_NEG_BIG = -1e30
_VMEM_LIMIT_BYTES = 56 * 1024 * 1024
_G_PITCH_PAD = 4
_BUILD_UNROLL = 32

_BF16 = jnp.bfloat16
_F32 = jnp.float32
_NT = (((1,), (1,)), ((), ()))


def _gelu(x):
    return 0.5 * x * (1.0 + lax.erf(x * _INV_SQRT2))


def _rms(x, g):
    ms = jnp.mean(x * x, axis=-1, keepdims=True)
    return x * lax.rsqrt(ms + _EPS) * g


def _dot(a, b):
    return jnp.dot(a, b, preferred_element_type=_F32)


def _dot_nt(a, b):
    return lax.dot_general(a, b, _NT, preferred_element_type=_F32)


def _mix_in_kernel(x_ref, g_ref, wqkv_ref, wft_ref, bf_ref, wug_ref, wvg_ref, wgate_ref,
                   lng_ref, lnb_ref, ws_ref, bexp_ref, wbb_ref,
                   q_ref, k_ref, v_ref, lf_ref, ga_ref, mb_ref, *, attn_w, d_model, chunk, q_scale):
    tm = x_ref.shape[0]
    h = _rms(x_ref[...], g_ref[...]).astype(_BF16)

    qkv = _dot(h, wqkv_ref[...])
    q_ref[...] = (qkv[:, :attn_w] * q_scale).astype(_BF16)
    k_ref[...] = qkv[:, attn_w:2 * attn_w].astype(_BF16)
    v_ref[...] = qkv[:, 2 * attn_w:].astype(_BF16)

    f_t = _dot_nt(wft_ref[...], h) + bf_ref[...]
    lf_ref[...] = jnp.minimum(f_t, 0.0) - jnp.log1p(jnp.exp(-jnp.abs(f_t)))

    ug = _gelu(_dot(h, wug_ref[...]))
    vg = _gelu(_dot(h, wvg_ref[...]))
    mu = jnp.mean(vg, axis=-1, keepdims=True)
    var = jnp.mean(jnp.square(vg - mu), axis=-1, keepdims=True)
    vn = (vg - mu) * lax.rsqrt(var + _EPS) * lng_ref[...] + lnb_ref[...]

    n_groups = ws_ref.shape[0]
    row = lax.broadcasted_iota(jnp.int32, (chunk, chunk), 0)
    col = lax.broadcasted_iota(jnp.int32, (chunk, chunk), 1)
    tril = (row >= col).astype(_F32)
    wm = [(ws_ref[g] * tril).astype(_BF16) for g in range(n_groups)]
    lane = lax.broadcasted_iota(jnp.int32, (chunk, _LANES), 1)
    lo_half = lane < (_LANES // 2)
    zs = []
    for c in range(tm // chunk):
        slabs = []
        for p in range(n_groups // 2):
            vp = vn[c * chunk:(c + 1) * chunk, p * _LANES:(p + 1) * _LANES]
            lo = jnp.where(lo_half, vp, 0.0).astype(_BF16)
            hi = jnp.where(lo_half, 0.0, vp).astype(_BF16)
            slabs.append(_dot(wm[2 * p], lo) + _dot(wm[2 * p + 1], hi))
        zs.append(jnp.concatenate(slabs, axis=1) + bexp_ref[...])
    z = jnp.concatenate(zs, axis=0) if len(zs) > 1 else zs[0]
    yb = (ug * z).astype(_BF16)

    gate = _dot(h, wgate_ref[...])
    ga_ref[...] = jax.nn.sigmoid(gate[:, :d_model]).astype(_BF16)
    mb = jax.nn.sigmoid(gate[:, d_model:]) * _dot(yb, wbb_ref[...])
    mb_ref[...] = mb.astype(_BF16)


def _mix_in(x2, g_mix, wqkv, wft, bf, wug, wvg, wgate, lng, lnb, ws, bexp, wbb, *, tm, chunk, q_scale):
    t, d = x2.shape
    attn_w = wqkv.shape[1] // 3
    n_heads = wft.shape[0]
    full = lambda a: pl.BlockSpec(a.shape, lambda i: (0,) * a.ndim)
    kern = functools.partial(_mix_in_kernel, attn_w=attn_w, d_model=d, chunk=chunk, q_scale=q_scale)
    return pl.pallas_call(
        kern,
        grid=(t // tm,),
        in_specs=[pl.BlockSpec((tm, d), lambda i: (i, 0)), full(g_mix), full(wqkv), full(wft), full(bf),
                  full(wug), full(wvg), full(wgate), full(lng), full(lnb), full(ws), full(bexp), full(wbb)],
        out_specs=[pl.BlockSpec((tm, attn_w), lambda i: (i, 0)),
                   pl.BlockSpec((tm, attn_w), lambda i: (i, 0)),
                   pl.BlockSpec((tm, attn_w), lambda i: (i, 0)),
                   pl.BlockSpec((n_heads, tm), lambda i: (0, i)),
                   pl.BlockSpec((tm, d), lambda i: (i, 0)),
                   pl.BlockSpec((tm, d), lambda i: (i, 0))],
        out_shape=[jax.ShapeDtypeStruct((t, attn_w), _BF16),
                   jax.ShapeDtypeStruct((t, attn_w), _BF16),
                   jax.ShapeDtypeStruct((t, attn_w), _BF16),
                   jax.ShapeDtypeStruct((n_heads, t), _F32),
                   jax.ShapeDtypeStruct((t, d), _BF16),
                   jax.ShapeDtypeStruct((t, d), _BF16)],
        compiler_params=pltpu.CompilerParams(dimension_semantics=("arbitrary",),
                                             vmem_limit_bytes=_VMEM_LIMIT_BYTES),
        name="mix_in",
    )(x2, g_mix, wqkv, wft, bf, wug, wvg, wgate, lng, lnb, ws, bexp, wbb)


def _cumsum_kernel(lf_ref, c_ref):
    n_rows, s = lf_ref.shape
    r = lax.broadcasted_iota(jnp.int32, (_LANES, _LANES), 0)
    c = lax.broadcasted_iota(jnp.int32, (_LANES, _LANES), 1)
    upper = (r <= c).astype(_F32)
    carry = jnp.zeros((n_rows, 1), _F32)
    for blk in range(s // _LANES):
        xb = lf_ref[:, blk * _LANES:(blk + 1) * _LANES]
        cb = jnp.dot(xb, upper, preferred_element_type=_F32, precision=lax.Precision.HIGHEST) + carry
        c_ref[:, blk * _LANES:(blk + 1) * _LANES] = cb
        carry = cb[:, _LANES - 1:_LANES]


def _cumsum(lf, *, seq):
    n_rows, t = lf.shape
    return pl.pallas_call(
        _cumsum_kernel,
        grid=(t // seq,),
        in_specs=[pl.BlockSpec((n_rows, seq), lambda b: (0, b))],
        out_specs=pl.BlockSpec((n_rows, seq), lambda b: (0, b)),
        out_shape=jax.ShapeDtypeStruct((n_rows, t), _F32),
        compiler_params=pltpu.CompilerParams(dimension_semantics=("arbitrary",)),
        name="cumsum_logf",
    )(lf)


def _attn_kernel(q_ref, k_ref, v_ref, c_ref, o_ref, *, tq, tk, head_dim):
    p = pl.program_id(1)
    qi = pl.program_id(2)
    q = q_ref[0]
    lane = lax.broadcasted_iota(jnp.int32, (tq, _LANES), 1)
    lo_half = lane < head_dim
    zero = jnp.zeros_like(q)
    causal = (lax.broadcasted_iota(jnp.int32, (tq, tk), 0) >= lax.broadcasted_iota(jnp.int32, (tq, tk), 1))
    heads = ((jnp.where(lo_half, q, zero), 2 * p), (jnp.where(lo_half, zero, q), 2 * p + 1))

    def scores(j, qh, head_row, diagonal):
        start = pl.multiple_of(j * tk, tk)
        cs = c_ref[pl.ds(head_row, 1), pl.ds(start, tk)]
        s = _dot_nt(qh, k_ref[0, pl.ds(start, tk), :]) - cs
        return jnp.where(causal, s, _NEG_BIG) if diagonal else s

    def lane_tiles(a):
        return [a[:, i * _LANES:(i + 1) * _LANES] for i in range(a.shape[1] // _LANES)]

    def max_block(j, carry, diagonal=False):
        out = []
        for (qh, head_row), mx in zip(heads, carry):
            for s_tile in lane_tiles(scores(j, qh, head_row, diagonal)):
                mx = jnp.maximum(mx, s_tile)
            out.append(mx)
        return tuple(out)

    def sum_block(j, carry, diagonal=False):
        start = pl.multiple_of(j * tk, tk)
        vb = v_ref[0, pl.ds(start, tk), :]
        out = []
        for (qh, head_row), m, (lsum, acc) in zip(heads, row_max, carry):
            pexp = jnp.exp(scores(j, qh, head_row, diagonal) - m)
            for p_tile in lane_tiles(pexp):
                lsum = lsum + p_tile
            out.append((lsum, acc + _dot(pexp.astype(_BF16), vb)))
        return tuple(out)

    def sweep(block_fn, init):
        carry = lax.fori_loop(0, qi // 2, lambda jj, c: block_fn(2 * jj + 1, block_fn(2 * jj, c)), init)
        carry = lax.cond(qi % 2 == 1, lambda c: block_fn(qi - 1, c), lambda c: c, carry)
        return block_fn(qi, carry, True)

    neg = jnp.full((tq, _LANES), _NEG_BIG, _F32)
    row_max = [jnp.max(mx, axis=-1, keepdims=True) for mx in sweep(max_block, (neg, neg))]
    zeros = jnp.zeros((tq, _LANES), _F32)
    (ls_lo, acc_lo), (ls_hi, acc_hi) = sweep(sum_block, ((zeros, zeros), (zeros, zeros)))
    o_lo = acc_lo / jnp.sum(ls_lo, axis=-1, keepdims=True)
    o_hi = acc_hi / jnp.sum(ls_hi, axis=-1, keepdims=True)
    o_ref[0] = jnp.where(lo_half, o_lo, o_hi).astype(_BF16)


def _attention(q, k, v, c, *, batch, seq, head_dim, tq, tk):
    t, attn_w = q.shape
    n_pairs = attn_w // _LANES
    n_heads = c.shape[0]
    assert tq == tk
    q3, k3, v3 =(a.reshape(batch, seq, attn_w) for a in (q, k, v))
    kern = functools.partial(_attn_kernel, tq=tq, tk=tk, head_dim=head_dim)
    out = pl.pallas_call(
        kern,
        grid=(batch, n_pairs, seq // tq),
        in_specs=[pl.BlockSpec((1, tq, _LANES), lambda b, p, i: (b, i, p)),
                  pl.BlockSpec((1, seq, _LANES), lambda b, p, i: (b, 0, p)),
                  pl.BlockSpec((1, seq, _LANES), lambda b, p, i: (b, 0, p)),
                  pl.BlockSpec((n_heads, seq), lambda b, p, i: (0, b))],
        out_specs=pl.BlockSpec((1, tq, _LANES), lambda b, p, i: (b, i, p)),
        out_shape=jax.ShapeDtypeStruct((batch, seq, attn_w), _BF16),
        compiler_params=pltpu.CompilerParams(dimension_semantics=("arbitrary", "arbitrary", "arbitrary"),
                                             vmem_limit_bytes=_VMEM_LIMIT_BYTES),
        name="forget_attention",
    )(q3, k3, v3, c)
    return out.reshape(t, attn_w)


def _extract_topk(vals, payload, n_out):
    n = vals.shape[0]
    pos = lax.broadcasted_iota(jnp.int32, vals.shape, 0)
    out_v, out_p = [], []
    for _ in range(n_out):
        m = jnp.max(vals, axis=0, keepdims=True)
        first = jnp.min(jnp.where(vals == m, pos, n), axis=0, keepdims=True)
        sel = pos == first
        out_v.append(m)
        out_p.append(jnp.max(jnp.where(sel, payload, -1), axis=0, keepdims=True))
        vals = jnp.where(sel, -jnp.inf, vals)
    return jnp.concatenate(out_v, axis=0), jnp.concatenate(out_p, axis=0)


def _pair_candidates(a0, a1, combine):
    half = _SUBLANES
    bc = lambda x, r: jnp.broadcast_to(x[r:r + 1], (half, x.shape[1]))
    parts = [combine(bc(a0, 0), a1[0:half]), combine(bc(a0, 0), a1[half:2 * half])]
    parts += [combine(bc(a0, a), a1[0:half]) for a in range(1, half)]
    parts.append(combine(a0[half:2 * half], bc(a1, 0)))
    return jnp.concatenate(parts, axis=0)


def _merge_route_kernel(ya_ref, ga_ref, mb_ref, x_ref, wba_ref, wout_ref, gffn_ref, wq_ref, sk_ref,
                        x1_ref, h2_ref, eidx_ref, gate_ref, gate_scr, eid_scr, *, n_keys):
    tm = x_ref.shape[0]
    n_heads = wq_ref.shape[0]
    ma = _dot(ya_ref[...], wba_ref[...])
    merged = (ga_ref[...].astype(_F32) * ma + mb_ref[...].astype(_F32)).astype(_BF16)
    x1 = x_ref[...] + _dot(merged, wout_ref[...])
    x1_ref[...] = x1
    h2_ref[...] = _rms(x1, gffn_ref[...]).astype(_BF16)

    def head_body(h, carry):
        qh = _dot(h2_ref[...], wq_ref[h])
        half = qh.shape[1] // 2
        s_t = [_dot_nt(sk_ref[h, pp], qh[:, pp * half:(pp + 1) * half].astype(_BF16)) for pp in range(2)]
        key_id = lax.broadcasted_iota(jnp.int32, (n_keys, _LANES), 0)
        for lg in range(tm // _LANES):
            ls = slice(lg * _LANES, (lg + 1) * _LANES)
            s0, i0 = _extract_topk(s_t[0][:, ls], key_id, _TOPK)
            s1, i1 = _extract_topk(s_t[1][:, ls], key_id, _TOPK)
            cand_s = _pair_candidates(s0, s1, lambda a, b: a + b)
            cand_e = _pair_candidates(i0, i1, lambda a, b: a * n_keys + b)
            best, eid = _extract_topk(cand_s, cand_e, _TOPK)
            ex = jnp.exp(best - best[0:1])
            gate = ex / jnp.sum(ex, axis=0, keepdims=True)
            r0 = pl.multiple_of(h * _TOPK, _TOPK)
            gate_scr[pl.ds(r0, _TOPK), ls] = gate
            eid_scr[pl.ds(r0, _TOPK), ls] = eid
        return carry

    lax.fori_loop(0, n_heads, head_body, 0)
    gate_ref[...] = gate_scr[...].T
    eidx_ref[...] = eid_scr[...].T


def _merge_route(ya, ga, mb, x2, wba, wout, gffn, wq, sk, *, tm):
    t, d = x2.shape
    attn_w = ya.shape[1]
    n_heads, _, n_keys, _ = sk.shape
    hk = n_heads * _TOPK
    full = lambda a: pl.BlockSpec(a.shape, lambda i: (0,) * a.ndim)
    kern = functools.partial(_merge_route_kernel, n_keys=n_keys)
    return pl.pallas_call(
        kern,
        grid=(t // tm,),
        in_specs=[pl.BlockSpec((tm, attn_w), lambda i: (i, 0)),
                  pl.BlockSpec((tm, d), lambda i: (i, 0)),
                  pl.BlockSpec((tm, d), lambda i: (i, 0)),
                  pl.BlockSpec((tm, d), lambda i: (i, 0)),
                  full(wba), full(wout), full(gffn), full(wq), full(sk)],
        out_specs=[pl.BlockSpec((tm, d), lambda i: (i, 0)),
                   pl.BlockSpec((tm, d), lambda i: (i, 0)),
                   pl.BlockSpec((tm, hk), lambda i: (i, 0)),
                   pl.BlockSpec((tm, hk), lambda i: (i, 0))],
        out_shape=[jax.ShapeDtypeStruct((t, d), _F32),
                   jax.ShapeDtypeStruct((t, d), _BF16),
                   jax.ShapeDtypeStruct((t, hk), jnp.int32),
                   jax.ShapeDtypeStruct((t, hk), _F32)],
        scratch_shapes=[pltpu.VMEM((hk, tm), _F32), pltpu.VMEM((hk, tm), jnp.int32)],
        compiler_params=pltpu.CompilerParams(dimension_semantics=("arbitrary",),
                                             vmem_limit_bytes=_VMEM_LIMIT_BYTES),
        name="merge_route",
    )(ya, ga, mb, x2, wba, wout, gffn, wq, sk)


def _peer_kernel(h2_ref, x1_ref, eidx_ref, gate_ref, u_ref, v_ref, gfin_ref, out_ref, gs_ref,
                 *, n_keys, pitch, final_norm):
    tb = h2_ref.shape[0]
    ec = u_ref.shape[0]
    e = pl.program_id(1)
    key_shift = n_keys.bit_length() - 1
    keys_per_step = ec // n_keys

    @pl.when(e == 0)
    def _build_gate_matrices():
        out_ref[...] = jnp.zeros_like(out_ref)
        row = lax.broadcasted_iota(jnp.int32, (n_keys, eidx_ref.shape[1]), 0)

        def build(tg, carry):
            for u in range(_BUILD_UNROLL):
                t = tg * _BUILD_UNROLL + u
                er = eidx_ref[pl.ds(t, 1), :]
                gr = gate_ref[pl.ds(t, 1), :]
                p_t = jnp.where((er >> key_shift) == row, gr, 0.0).astype(_BF16)
                q_t = jnp.where((er & (n_keys - 1)) == row, 1.0, 0.0).astype(_BF16)
                gs_ref[pl.ds(t * pitch, n_keys), :] = _dot_nt(p_t, q_t)
            return carry

        lax.fori_loop(0, tb // _BUILD_UNROLL, build, 0)

    a = _dot_nt(h2_ref[...], u_ref[...])
    ws = []
    for ii in range(keys_per_step):
        gi = gs_ref[pl.ds(e * keys_per_step + ii, tb, stride=pitch), :]
        ws.append((gi * _gelu(a[:, ii * n_keys:(ii + 1) * n_keys])).astype(_BF16))
    w = jnp.concatenate(ws, axis=1) if len(ws) > 1 else ws[0]
    out_ref[...] += _dot(w, v_ref[...])

    @pl.when(e == pl.num_programs(1) - 1)
    def _finish():
        y = x1_ref[...] + out_ref[...]
        out_ref[...] = _rms(y, gfin_ref[...]) if final_norm else y


def _peer(h2, x1, eidx, gate, u_bf, v_bf, gfin, *, tb, ec, n_keys, final_norm):
    t, d = x1.shape
    n_exp = u_bf.shape[0]
    hk = eidx.shape[1]
    pitch = n_keys + _G_PITCH_PAD
    assert tb % _BUILD_UNROLL == 0 and ec % n_keys == 0
    kern = functools.partial(_peer_kernel, n_keys=n_keys, pitch=pitch, final_norm=final_norm)
    once = dict(pipeline_mode=pl.Buffered(1))
    return pl.pallas_call(
        kern,
        grid=(t // tb, n_exp // ec),
        in_specs=[pl.BlockSpec((tb, d), lambda i, e: (i, 0), **once),
                  pl.BlockSpec((tb, d), lambda i, e: (i, 0), **once),
                  pl.BlockSpec((tb, hk), lambda i, e: (i, 0), **once),
                  pl.BlockSpec((tb, hk), lambda i, e: (i, 0), **once),
                  pl.BlockSpec((ec, d), lambda i, e: (e, 0)),
                  pl.BlockSpec((ec, d), lambda i, e: (e, 0)),
                  pl.BlockSpec((1, d), lambda i, e: (0, 0), **once)],
        out_specs=pl.BlockSpec((tb, d), lambda i, e: (i, 0), **once),
        out_shape=jax.ShapeDtypeStruct((t, d), _F32),
        scratch_shapes=[pltpu.VMEM((tb * pitch, n_keys), _F32)],
        compiler_params=pltpu.CompilerParams(dimension_semantics=("arbitrary", "arbitrary"),
                                             vmem_limit_bytes=_VMEM_LIMIT_BYTES),
        name="peer_dense",
    )(h2, x1, eidx, gate, u_bf, v_bf, gfin)


def _tile(n, want):
    t = min(n, want)
    while n % t:
        t -= _LANES
    return t


def kernel(x, norm_mix_g, w_in, b_f, sgu_ln_g, sgu_ln_b, w_s, b_s, w_branch_a, w_branch_b, w_out, norm_ffn_g,
           w_query, sub_keys, expert_u, expert_v, norm_final_g):
    batch, seq, d = x.shape
    depth = w_in.shape[0]
    n_attn_heads = b_f.shape[1]
    attn_w = w_branch_a.shape[1]
    head_dim = attn_w // n_attn_heads
    sgu_w = w_branch_b.shape[1]
    n_groups, chunk = w_s.shape[1], w_s.shape[2]
    group_dim = sgu_w // n_groups
    peer_heads, _, n_keys, half = sub_keys.shape[1:]
    assert 2 * head_dim == _LANES and 2 * group_dim == _LANES and chunk == _LANES
    assert n_keys == _LANES and half == _LANES and peer_heads * _TOPK == _LANES
    assert seq % chunk == 0 and d % _LANES == 0
    off_f = 3 * attn_w
    off_ug = off_f + n_attn_heads
    off_vg = off_ug + sgu_w
    off_gate = off_vg + sgu_w
    assert w_in.shape[2] == off_gate + 2 * d

    t = batch * seq
    tm = _tile(seq, 256)
    x2 = x.reshape(t, d)
    for l in range(depth):
        wl = w_in[l]
        wqkv = wl[:, :off_f].astype(_BF16)
        wft = wl[:, off_f:off_ug].T.astype(_BF16)
        wug = wl[:, off_ug:off_vg].astype(_BF16)
        wvg = wl[:, off_vg:off_gate].astype(_BF16)
        wgate = wl[:, off_gate:].astype(_BF16)
        bexp = jnp.repeat(b_s[l].T, group_dim, axis=1)
        q, k, v, lf, ga, mb = _mix_in(
            x2, norm_mix_g[l][None], wqkv, wft, b_f[l][:, None], wug, wvg, wgate,
            sgu_ln_g[l][None], sgu_ln_b[l][None], w_s[l], bexp, w_branch_b[l].astype(_BF16),
            tm=tm, chunk=chunk, q_scale=head_dim ** -0.5)
        c = _cumsum(lf, seq=seq)
        ya = _attention(q, k, v, c, batch=batch, seq=seq, head_dim=head_dim, tq=tm, tk=tm)
        wq = w_query[l].reshape(d, peer_heads, 2 * half).transpose(1, 0, 2).astype(_BF16)
        x1, h2, eidx, gate = _merge_route(
            ya, ga, mb, x2, w_branch_a[l].astype(_BF16), w_out[l].astype(_BF16), norm_ffn_g[l][None],
            wq, sub_keys[l].astype(_BF16), tm=tm)
        x2 = _peer(h2, x1, eidx, gate, expert_u[l].astype(_BF16), expert_v[l].astype(_BF16),
                   norm_final_g[None], tb=_tile(t, 512), ec=_tile(n_keys * n_keys, 1024), n_keys=n_keys,
                   final_norm=(l == depth - 1))
    return x2.reshape(batch, seq, d)
```

```python
import functools
import math

import jax
import jax.numpy as jnp
from jax import lax
from jax.experimental import pallas as pl
from jax.experimental.pallas import tpu as pltpu

_EPS = 1e-6
_TOPK = 16
_LANES = 128
_SUBLANES = 8
_INV_SQRT2 = 1.0 / math.sqrt(2.0)
_NEG_BIG = -1e30
_VMEM_LIMIT_BYTES = 56 * 1024 * 1024
_G_PITCH_PAD = 4
_ROUTE_CHUNK = 2
_BUILD_UNROLL = 32

_BF16 = jnp.bfloat16
_F32 = jnp.float32
_NT = (((1,), (1,)), ((), ()))


def _gelu(x):
    return 0.5 * x * (1.0 + lax.erf(x * _INV_SQRT2))


def _rms(x, g):
    ms = jnp.mean(x * x, axis=-1, keepdims=True)
    return x * lax.rsqrt(ms + _EPS) * g


def _dot(a, b):
    return jnp.dot(a, b, preferred_element_type=_F32)


def _dot_nt(a, b):
    return lax.dot_general(a, b, _NT, preferred_element_type=_F32)


def _mix_in_kernel(x_ref, g_ref, wqkv_ref, wft_ref, bf_ref, wug_ref, wvg_ref, wgate_ref,
                   lng_ref, lnb_ref, ws_ref, bexp_ref, wbb_ref,
                   q_ref, k_ref, v_ref, lf_ref, ga_ref, mb_ref, *, attn_w, d_model, chunk, q_scale):
    tm = x_ref.shape[0]
    h = _rms(x_ref[...], g_ref[...]).astype(_BF16)

    qkv = _dot(h, wqkv_ref[...])
    q_ref[...] = (qkv[:, :attn_w] * q_scale).astype(_BF16)
    k_ref[...] = qkv[:, attn_w:2 * attn_w].astype(_BF16)
    v_ref[...] = qkv[:, 2 * attn_w:].astype(_BF16)

    f_t = _dot_nt(wft_ref[...], h) + bf_ref[...]
    lf_ref[...] = jnp.minimum(f_t, 0.0) - jnp.log1p(jnp.exp(-jnp.abs(f_t)))

    ug = _gelu(_dot(h, wug_ref[...]))
    vg = _gelu(_dot(h, wvg_ref[...]))
    mu = jnp.mean(vg, axis=-1, keepdims=True)
    var = jnp.mean(jnp.square(vg - mu), axis=-1, keepdims=True)
    vn = (vg - mu) * lax.rsqrt(var + _EPS) * lng_ref[...] + lnb_ref[...]

    n_groups = ws_ref.shape[0]
    row = lax.broadcasted_iota(jnp.int32, (chunk, chunk), 0)
    col = lax.broadcasted_iota(jnp.int32, (chunk, chunk), 1)
    tril = (row >= col).astype(_F32)
    wm = [(ws_ref[g] * tril).astype(_BF16) for g in range(n_groups)]
    lane = lax.broadcasted_iota(jnp.int32, (chunk, _LANES), 1)
    lo_half = lane < (_LANES // 2)
    zs = []
    for c in range(tm // chunk):
        slabs = []
        for p in range(n_groups // 2):
            vp = vn[c * chunk:(c + 1) * chunk, p * _LANES:(p + 1) * _LANES]
            lo = jnp.where(lo_half, vp, 0.0).astype(_BF16)
            hi = jnp.where(lo_half, 0.0, vp).astype(_BF16)
            slabs.append(_dot(wm[2 * p], lo) + _dot(wm[2 * p + 1], hi))
        zs.append(jnp.concatenate(slabs, axis=1) + bexp_ref[...])
    z = jnp.concatenate(zs, axis=0) if len(zs) > 1 else zs[0]
    yb = (ug * z).astype(_BF16)

    gate = _dot(h, wgate_ref[...])
    ga_ref[...] = jax.nn.sigmoid(gate[:, :d_model]).astype(_BF16)
    mb = jax.nn.sigmoid(gate[:, d_model:]) * _dot(yb, wbb_ref[...])
    mb_ref[...] = mb.astype(_BF16)


def _mix_in(x2, g_mix, wqkv, wft, bf, wug, wvg, wgate, lng, lnb, ws, bexp, wbb, *, tm, chunk, q_scale):
    t, d = x2.shape
    attn_w = wqkv.shape[1] // 3
    n_heads = wft.shape[0]
    full = lambda a: pl.BlockSpec(a.shape, lambda i: (0,) * a.ndim)
    kern = functools.partial(_mix_in_kernel, attn_w=attn_w, d_model=d, chunk=chunk, q_scale=q_scale)
    return pl.pallas_call(
        kern,
        grid=(t // tm,),
        in_specs=[pl.BlockSpec((tm, d), lambda i: (i, 0)), full(g_mix), full(wqkv), full(wft), full(bf),
                  full(wug), full(wvg), full(wgate), full(lng), full(lnb), full(ws), full(bexp), full(wbb)],
        out_specs=[pl.BlockSpec((tm, attn_w), lambda i: (i, 0)),
                   pl.BlockSpec((tm, attn_w), lambda i: (i, 0)),
                   pl.BlockSpec((tm, attn_w), lambda i: (i, 0)),
                   pl.BlockSpec((n_heads, tm), lambda i: (0, i)),
                   pl.BlockSpec((tm, d), lambda i: (i, 0)),
                   pl.BlockSpec((tm, d), lambda i: (i, 0))],
        out_shape=[jax.ShapeDtypeStruct((t, attn_w), _BF16),
                   jax.ShapeDtypeStruct((t, attn_w), _BF16),
                   jax.ShapeDtypeStruct((t, attn_w), _BF16),
                   jax.ShapeDtypeStruct((n_heads, t), _F32),
                   jax.ShapeDtypeStruct((t, d), _BF16),
                   jax.ShapeDtypeStruct((t, d), _BF16)],
        compiler_params=pltpu.CompilerParams(dimension_semantics=("arbitrary",),
                                             vmem_limit_bytes=_VMEM_LIMIT_BYTES),
        name="mix_in",
    )(x2, g_mix, wqkv, wft, bf, wug, wvg, wgate, lng, lnb, ws, bexp, wbb)


def _cumsum_kernel(lf_ref, c_ref):
    n_rows, s = lf_ref.shape
    r = lax.broadcasted_iota(jnp.int32, (_LANES, _LANES), 0)
    c = lax.broadcasted_iota(jnp.int32, (_LANES, _LANES), 1)
    upper = (r <= c).astype(_F32)
    carry = jnp.zeros((n_rows, 1), _F32)
    for blk in range(s // _LANES):
        xb = lf_ref[:, blk * _LANES:(blk + 1) * _LANES]
        cb = jnp.dot(xb, upper, preferred_element_type=_F32, precision=lax.Precision.HIGHEST) + carry
        c_ref[:, blk * _LANES:(blk + 1) * _LANES] = cb
        carry = cb[:, _LANES - 1:_LANES]


def _cumsum(lf, *, seq):
    n_rows, t = lf.shape
    return pl.pallas_call(
        _cumsum_kernel,
        grid=(t // seq,),
        in_specs=[pl.BlockSpec((n_rows, seq), lambda b: (0, b))],
        out_specs=pl.BlockSpec((n_rows, seq), lambda b: (0, b)),
        out_shape=jax.ShapeDtypeStruct((n_rows, t), _F32),
        compiler_params=pltpu.CompilerParams(dimension_semantics=("arbitrary",)),
        name="cumsum_logf",
    )(lf)


def _attn_kernel(q_ref, k_ref, v_ref, c_ref, o_ref, *, tq, tk, head_dim):
    p = pl.program_id(1)
    qi = pl.program_id(2)
    q = q_ref[0]
    lane = lax.broadcasted_iota(jnp.int32, (tq, _LANES), 1)
    lo_half = lane < head_dim
    zero = jnp.zeros_like(q)
    causal = (lax.broadcasted_iota(jnp.int32, (tq, tk), 0) >= lax.broadcasted_iota(jnp.int32, (tq, tk), 1))
    heads = ((jnp.where(lo_half, q, zero), 2 * p), (jnp.where(lo_half, zero, q), 2 * p + 1))

    def scores(j, qh, head_row, diagonal):
        start = pl.multiple_of(j * tk, tk)
        cs = c_ref[pl.ds(head_row, 1), pl.ds(start, tk)]
        s = _dot_nt(qh, k_ref[0, pl.ds(start, tk), :]) - cs
        return jnp.where(causal, s, _NEG_BIG) if diagonal else s

    def lane_tiles(a):
        return [a[:, i * _LANES:(i + 1) * _LANES] for i in range(a.shape[1] // _LANES)]

    def max_block(j, carry, diagonal=False):
        out = []
        for (qh, head_row), mx in zip(heads, carry):
            for s_tile in lane_tiles(scores(j, qh, head_row, diagonal)):
                mx = jnp.maximum(mx, s_tile)
            out.append(mx)
        return tuple(out)

    def sum_block(j, carry, diagonal=False):
        start = pl.multiple_of(j * tk, tk)
        vb = v_ref[0, pl.ds(start, tk), :]
        out = []
        for (qh, head_row), m, (lsum, acc) in zip(heads, row_max, carry):
            pexp = jnp.exp(scores(j, qh, head_row, diagonal) - m)
            for p_tile in lane_tiles(pexp):
                lsum = lsum + p_tile
            out.append((lsum, acc + _dot(pexp.astype(_BF16), vb)))
        return tuple(out)

    def sweep(block_fn, init):
        carry = lax.fori_loop(0, qi // 2, lambda jj, c: block_fn(2 * jj + 1, block_fn(2 * jj, c)), init)
        carry = lax.cond(qi % 2 == 1, lambda c: block_fn(qi - 1, c), lambda c: c, carry)
        return block_fn(qi, carry, True)

    neg = jnp.full((tq, _LANES), _NEG_BIG, _F32)
    row_max = [jnp.max(mx, axis=-1, keepdims=True) for mx in sweep(max_block, (neg, neg))]
    zeros = jnp.zeros((tq, _LANES), _F32)
    (ls_lo, acc_lo), (ls_hi, acc_hi) = sweep(sum_block, ((zeros, zeros), (zeros, zeros)))
    o_lo = acc_lo / jnp.sum(ls_lo, axis=-1, keepdims=True)
    o_hi = acc_hi / jnp.sum(ls_hi, axis=-1, keepdims=True)
    o_ref[0] = jnp.where(lo_half, o_lo, o_hi).astype(_BF16)


def _attention(q, k, v, c, *, batch, seq, head_dim, tq, tk):
    t, attn_w = q.shape
    n_pairs = attn_w // _LANES
    n_heads = c.shape[0]
    assert tq == tk
    q3, k3, v3 =(a.reshape(batch, seq, attn_w) for a in (q, k, v))
    kern = functools.partial(_attn_kernel, tq=tq, tk=tk, head_dim=head_dim)
    out = pl.pallas_call(
        kern,
        grid=(batch, n_pairs, seq // tq),
        in_specs=[pl.BlockSpec((1, tq, _LANES), lambda b, p, i: (b, i, p)),
                  pl.BlockSpec((1, seq, _LANES), lambda b, p, i: (b, 0, p)),
                  pl.BlockSpec((1, seq, _LANES), lambda b, p, i: (b, 0, p)),
                  pl.BlockSpec((n_heads, seq), lambda b, p, i: (0, b))],
        out_specs=pl.BlockSpec((1, tq, _LANES), lambda b, p, i: (b, i, p)),
        out_shape=jax.ShapeDtypeStruct((batch, seq, attn_w), _BF16),
        compiler_params=pltpu.CompilerParams(dimension_semantics=("arbitrary", "arbitrary", "arbitrary"),
                                             vmem_limit_bytes=_VMEM_LIMIT_BYTES),
        name="forget_attention",
    )(q3, k3, v3, c)
    return out.reshape(t, attn_w)


def _merge_kernel(ya_ref, ga_ref, mb_ref, x_ref, wba_ref, wout_ref, gffn_ref, x1_ref, h2_ref):
    ma = _dot(ya_ref[...], wba_ref[...])
    merged = (ga_ref[...].astype(_F32) * ma + mb_ref[...].astype(_F32)).astype(_BF16)
    x1 = x_ref[...] + _dot(merged, wout_ref[...])
    x1_ref[...] = x1
    h2_ref[...] = _rms(x1, gffn_ref[...]).astype(_BF16)


def _merge(ya, ga, mb, x2, wba, wout, gffn, *, tm):
    t, d = x2.shape
    attn_w = ya.shape[1]
    full = lambda a: pl.BlockSpec(a.shape, lambda i: (0,) * a.ndim)
    return pl.pallas_call(
        _merge_kernel,
        grid=(t // tm,),
        in_specs=[pl.BlockSpec((tm, attn_w), lambda i: (i, 0)),
                  pl.BlockSpec((tm, d), lambda i: (i, 0)),
                  pl.BlockSpec((tm, d), lambda i: (i, 0)),
                  pl.BlockSpec((tm, d), lambda i: (i, 0)),
                  full(wba), full(wout), full(gffn)],
        out_specs=[pl.BlockSpec((tm, d), lambda i: (i, 0)),
                   pl.BlockSpec((tm, d), lambda i: (i, 0))],
        out_shape=[jax.ShapeDtypeStruct((t, d), _F32),
                   jax.ShapeDtypeStruct((t, d), _BF16)],
        compiler_params=pltpu.CompilerParams(dimension_semantics=("arbitrary",),
                                             vmem_limit_bytes=_VMEM_LIMIT_BYTES),
        name="merge_out",
    )(ya, ga, mb, x2, wba, wout, gffn)


def _compare_exchange(a, b):
    if a is None:
        return b, None
    if b is None:
        return a, None
    a_first = a[0] >= b[0]
    hi = (jnp.where(a_first, a[0], b[0]), jnp.where(a_first, a[1], b[1]))
    lo = (jnp.where(a_first, b[0], a[0]), jnp.where(a_first, b[1], a[1]))
    return hi, lo


def _odd_even_merge_sort_pairs(n):
    pairs = []
    p = 1
    while p < n:
        k = p
        while k >= 1:
            for j in range(k % p, n - k, 2 * k):
                for i in range(min(k, n - j - k)):
                    if (i + j) // (2 * p) == (i + j + k) // (2 * p):
                        pairs.append((i + j, i + j + k))
            k //= 2
        p *= 2
    return pairs


def _sort_desc(items):
    items = list(items)
    for i, j in _odd_even_merge_sort_pairs(len(items)):
        items[i], items[j] = _compare_exchange(items[i], items[j])
    return items


def _merge_top(a, b, n_out, dropped):
    a = a + [None] * (n_out - len(a))
    b = b + [None] * (n_out - len(b))
    kept = []
    for k in range(n_out):
        hi, lo = _compare_exchange(a[k], b[n_out - 1 - k])
        kept.append(hi)
        if lo is not None:
            dropped.append(lo[0])
    d = n_out // 2
    while d >= 1:
        for i in range(n_out):
            if i & d == 0:
                kept[i], kept[i + d] = _compare_exchange(kept[i], kept[i + d])
        d //= 2
    return kept


def _max_of(values):
    values = list(values)
    while len(values) > 1:
        values = [jnp.maximum(values[i], values[i + 1]) for i in range(0, len(values) - 1, 2)] + \
                 ([values[-1]] if len(values) % 2 else [])
    return values[0]


def _network_top(entries, group, n_out):
    dropped = []
    lists = [_sort_desc(entries[g:g + group]) for g in range(0, len(entries), group)]
    while len(lists) > 1:
        lists = [_merge_top(lists[i], lists[i + 1], n_out, dropped) for i in range(0, len(lists), 2)]
    return lists[0], _max_of(dropped)


def _network_pair_top(top0, top1, n_keys):
    pair = lambda a, b: (top0[a][0] + top1[b][0], top0[a][1] * n_keys + top1[b][1])
    rows = [[pair(a, b) for b in range(_TOPK // (a + 1))] for a in range(_SUBLANES)]
    col = [pair(a, 0) for a in range(_SUBLANES, _TOPK)]
    dropped = []
    merge = lambda x, y: _merge_top(x, y, _TOPK, dropped)
    small = merge(rows[4], merge(rows[5], merge(rows[6], rows[7])))
    best = merge(merge(rows[1], col), merge(rows[2], rows[3]))
    best = merge(merge(best, small), rows[0])
    return best, _max_of(dropped)


def _mark_ties(flag, entries, dropped_max, inner):
    flag = jnp.where(entries[-1][0] == dropped_max, 1.0, flag)
    if inner:
        for x, y in zip(entries[:-1], entries[1:]):
            flag = jnp.where(x[0] == y[0], 1.0, flag)
    return flag


def _extract_topk(vals, payload, n_out):
    n = vals.shape[0]
    pos = lax.broadcasted_iota(jnp.int32, vals.shape, 0)
    out_v, out_p = [], []
    for _ in range(n_out):
        m = jnp.max(vals, axis=0, keepdims=True)
        first = jnp.min(jnp.where(vals == m, pos, n), axis=0, keepdims=True)
        sel = pos == first
        out_v.append(m)
        out_p.append(jnp.max(jnp.where(sel, payload, -1), axis=0, keepdims=True))
        vals = jnp.where(sel, -jnp.inf, vals)
    return jnp.concatenate(out_v, axis=0), jnp.concatenate(out_p, axis=0)


def _pair_candidates(a0, a1, combine):
    half = _SUBLANES
    bc = lambda x, r: jnp.broadcast_to(x[r:r + 1], (half, x.shape[1]))
    parts = [combine(bc(a0, 0), a1[0:half]), combine(bc(a0, 0), a1[half:2 * half])]
    parts += [combine(bc(a0, a), a1[0:half]) for a in range(1, half)]
    parts.append(combine(a0[half:2 * half], bc(a1, 0)))
    return jnp.concatenate(parts, axis=0)


def _route_kernel(h2_ref, wq_ref, sk_ref, eidx_ref, gate_ref, s_scr, gate_scr, eid_scr, *, n_keys, pitch):
    tm = h2_ref.shape[0]
    n_heads = wq_ref.shape[0]
    nj = tm // _LANES

    def head_body(h, carry):
        half = wq_ref.shape[2] // 2
        for c in range(0, nj, _ROUTE_CHUNK):
            rows = slice(c * _LANES, min(c + _ROUTE_CHUNK, nj) * _LANES)
            qh = _dot(h2_ref[rows, :], wq_ref[h]).astype(_BF16)
            for pp in range(2):
                s_t = _dot_nt(sk_ref[h, pp], qh[:, pp * half:(pp + 1) * half])
                for jj in range(s_t.shape[1] // _LANES):
                    j = c + jj
                    s_scr[pp, j * pitch:j * pitch + n_keys, :] = s_t[:, jj * _LANES:(jj + 1) * _LANES]

        tops = []
        tie = jnp.zeros((nj, _LANES), _F32)
        for pp in range(2):
            keys = [(s_scr[pp, pl.ds(k, nj, stride=pitch), :], k) for k in range(n_keys)]
            top, dropped_max = _network_top(keys, _TOPK, _TOPK)
            tops.append(top)
            tie = _mark_ties(tie, top, dropped_max, inner=True)
        best, dropped_max = _network_pair_top(tops[0], tops[1], n_keys)
        tie = _mark_ties(tie, best, dropped_max, inner=False)
        ex = [jnp.exp(e[0] - best[0][0]) for e in best]
        denom = ex[0]
        for x in ex[1:]:
            denom = denom + x
        for r in range(_TOPK):
            gate_scr[pl.ds(h * _TOPK + r, nj, stride=pitch), :] = ex[r] / denom
            eid_scr[pl.ds(h * _TOPK + r, nj, stride=pitch), :] = best[r][1]
        any_tie = jnp.max(tie) > 0.0

        @pl.when(any_tie)
        def _exact_path():
            key_id = lax.broadcasted_iota(jnp.int32, (n_keys, _LANES), 0)

            def group(j, c):
                base = j * pitch
                e0, k0 = _extract_topk(s_scr[0, pl.ds(base, n_keys), :], key_id, _TOPK)
                e1, k1 = _extract_topk(s_scr[1, pl.ds(base, n_keys), :], key_id, _TOPK)
                cand_s = _pair_candidates(e0, e1, lambda a, b: a + b)
                cand_e = _pair_candidates(k0, k1, lambda a, b: a * n_keys + b)
                top_s, top_e = _extract_topk(cand_s, cand_e, _TOPK)
                p = jnp.exp(top_s - top_s[0:1])
                gate_scr[pl.ds(base + h * _TOPK, _TOPK), :] = p / jnp.sum(p, axis=0, keepdims=True)
                eid_scr[pl.ds(base + h * _TOPK, _TOPK), :] = top_e
                return c

            lax.fori_loop(0, nj, group, 0)

        return carry

    lax.fori_loop(0, n_heads, head_body, 0)
    for j in range(nj):
        gate_ref[j * _LANES:(j + 1) * _LANES, :] = gate_scr[j * pitch:j * pitch + n_keys, :].T
        eidx_ref[j * _LANES:(j + 1) * _LANES, :] = eid_scr[j * pitch:j * pitch + n_keys, :].T


def _route(h2, wq, sk, *, tm):
    t, d = h2.shape
    n_heads, _, n_keys, _ = sk.shape
    hk = n_heads * _TOPK
    assert hk == n_keys
    pitch = n_keys + _G_PITCH_PAD
    rows = tm // _LANES * pitch
    full = lambda a: pl.BlockSpec(a.shape, lambda i: (0,) * a.ndim)
    kern = functools.partial(_route_kernel, n_keys=n_keys, pitch=pitch)
    return pl.pallas_call(
        kern,
        grid=(t // tm,),
        in_specs=[pl.BlockSpec((tm, d), lambda i: (i, 0)), full(wq), full(sk)],
        out_specs=[pl.BlockSpec((tm, hk), lambda i: (i, 0)),
                   pl.BlockSpec((tm, hk), lambda i: (i, 0))],
        out_shape=[jax.ShapeDtypeStruct((t, hk), jnp.int32),
                   jax.ShapeDtypeStruct((t, hk), _F32)],
        scratch_shapes=[pltpu.VMEM((2, rows, _LANES), _F32), pltpu.VMEM((rows, _LANES), _F32),
                        pltpu.VMEM((rows, _LANES), jnp.int32)],
        compiler_params=pltpu.CompilerParams(dimension_semantics=("arbitrary",),
                                             vmem_limit_bytes=_VMEM_LIMIT_BYTES),
        name="route_topk",
    )(h2, wq, sk)


def _peer_kernel(h2_ref, x1_ref, eidx_ref, gate_ref, u_ref, v_ref, gfin_ref, out_ref, gs_ref,
                 *, n_keys, pitch, final_norm):
    tb = h2_ref.shape[0]
    ec = u_ref.shape[0]
    e = pl.program_id(1)
    key_shift = n_keys.bit_length() - 1
    keys_per_step = ec // n_keys

    @pl.when(e == 0)
    def _build_gate_matrices():
        out_ref[...] = jnp.zeros_like(out_ref)
        row = lax.broadcasted_iota(jnp.int32, (n_keys, eidx_ref.shape[1]), 0)

        def build(tg, carry):
            for u in range(_BUILD_UNROLL):
                t = tg * _BUILD_UNROLL + u
                er = eidx_ref[pl.ds(t, 1), :]
                gr = gate_ref[pl.ds(t, 1), :]
                p_t = jnp.where((er >> key_shift) == row, gr, 0.0).astype(_BF16)
                q_t = jnp.where((er & (n_keys - 1)) == row, 1.0, 0.0).astype(_BF16)
                gs_ref[pl.ds(t * pitch, n_keys), :] = _dot_nt(p_t, q_t)
            return carry

        lax.fori_loop(0, tb // _BUILD_UNROLL, build, 0)

    a = _dot_nt(h2_ref[...], u_ref[...])
    ws = []
    for ii in range(keys_per_step):
        gi = gs_ref[pl.ds(e * keys_per_step + ii, tb, stride=pitch), :]
        ws.append((gi * _gelu(a[:, ii * n_keys:(ii + 1) * n_keys])).astype(_BF16))
    w = jnp.concatenate(ws, axis=1) if len(ws) > 1 else ws[0]
    out_ref[...] += _dot(w, v_ref[...])

    @pl.when(e == pl.num_programs(1) - 1)
    def _finish():
        y = x1_ref[...] + out_ref[...]
        out_ref[...] = _rms(y, gfin_ref[...]) if final_norm else y


def _peer(h2, x1, eidx, gate, u_bf, v_bf, gfin, *, tb, ec, n_keys, final_norm):
    t, d = x1.shape
    n_exp = u_bf.shape[0]
    hk = eidx.shape[1]
    pitch = n_keys + _G_PITCH_PAD
    assert tb % _BUILD_UNROLL == 0 and ec % n_keys == 0
    kern = functools.partial(_peer_kernel, n_keys=n_keys, pitch=pitch, final_norm=final_norm)
    once = dict(pipeline_mode=pl.Buffered(1))
    return pl.pallas_call(
        kern,
        grid=(t // tb, n_exp // ec),
        in_specs=[pl.BlockSpec((tb, d), lambda i, e: (i, 0), **once),
                  pl.BlockSpec((tb, d), lambda i, e: (i, 0), **once),
                  pl.BlockSpec((tb, hk), lambda i, e: (i, 0), **once),
                  pl.BlockSpec((tb, hk), lambda i, e: (i, 0), **once),
                  pl.BlockSpec((ec, d), lambda i, e: (e, 0)),
                  pl.BlockSpec((ec, d), lambda i, e: (e, 0)),
                  pl.BlockSpec((1, d), lambda i, e: (0, 0), **once)],
        out_specs=pl.BlockSpec((tb, d), lambda i, e: (i, 0), **once),
        out_shape=jax.ShapeDtypeStruct((t, d), _F32),
        scratch_shapes=[pltpu.VMEM((tb * pitch, n_keys), _F32)],
        compiler_params=pltpu.CompilerParams(dimension_semantics=("arbitrary", "arbitrary"),
                                             vmem_limit_bytes=_VMEM_LIMIT_BYTES),
        name="peer_dense",
    )(h2, x1, eidx, gate, u_bf, v_bf, gfin)


def _tile(n, want):
    t = min(n, want)
    while n % t:
        t -= _LANES
    return t


def kernel(x, norm_mix_g, w_in, b_f, sgu_ln_g, sgu_ln_b, w_s, b_s, w_branch_a, w_branch_b, w_out, norm_ffn_g,
           w_query, sub_keys, expert_u, expert_v, norm_final_g):
    batch, seq, d = x.shape
    depth = w_in.shape[0]
    n_attn_heads = b_f.shape[1]
    attn_w = w_branch_a.shape[1]
    head_dim = attn_w // n_attn_heads
    sgu_w = w_branch_b.shape[1]
    n_groups, chunk = w_s.shape[1], w_s.shape[2]
    group_dim = sgu_w // n_groups
    peer_heads, _, n_keys, half = sub_keys.shape[1:]
    assert 2 * head_dim == _LANES and 2 * group_dim == _LANES and chunk == _LANES
    assert n_keys == _LANES and half == _LANES and peer_heads * _TOPK == _LANES
    assert seq % chunk == 0 and d % _LANES == 0
    off_f = 3 * attn_w
    off_ug = off_f + n_attn_heads
    off_vg = off_ug + sgu_w
    off_gate = off_vg + sgu_w
    assert w_in.shape[2] == off_gate + 2 * d

    t = batch * seq
    tm = _tile(seq, 256)
    x2 = x.reshape(t, d)
    for l in range(depth):
        wl = w_in[l]
        wqkv = wl[:, :off_f].astype(_BF16)
        wft = wl[:, off_f:off_ug].T.astype(_BF16)
        wug = wl[:, off_ug:off_vg].astype(_BF16)
        wvg = wl[:, off_vg:off_gate].astype(_BF16)
        wgate = wl[:, off_gate:].astype(_BF16)
        bexp = jnp.repeat(b_s[l].T, group_dim, axis=1)
        q, k, v, lf, ga, mb = _mix_in(
            x2, norm_mix_g[l][None], wqkv, wft, b_f[l][:, None], wug, wvg, wgate,
            sgu_ln_g[l][None], sgu_ln_b[l][None], w_s[l], bexp, w_branch_b[l].astype(_BF16),
            tm=tm, chunk=chunk, q_scale=head_dim ** -0.5)
        c = _cumsum(lf, seq=seq)
        ya = _attention(q, k, v, c, batch=batch, seq=seq, head_dim=head_dim, tq=tm, tk=tm)
        wq = w_query[l].reshape(d, peer_heads, 2 * half).transpose(1, 0, 2).astype(_BF16)
        x1, h2 = _merge(ya, ga, mb, x2, w_branch_a[l].astype(_BF16), w_out[l].astype(_BF16),
                        norm_ffn_g[l][None], tm=tm)
        eidx, gate = _route(h2, wq, sub_keys[l].astype(_BF16), tm=_tile(t, 8 * _LANES))
        x2 = _peer(h2, x1, eidx, gate, expert_u[l].astype(_BF16), expert_v[l].astype(_BF16),
                   norm_final_g[None], tb=_tile(t, 512), ec=_tile(n_keys * n_keys, 1024), n_keys=n_keys,
                   final_norm=(l == depth - 1))
    return x2.reshape(batch, seq, d)
```
